```python
import math
import jax, jax.numpy as jnp
from jax import lax
import numpy as np

D_MODEL = 1024
BATCH = 2
SEQ = 8192
DEPTH = 2

DN_ALPHA = (2 * DEPTH) ** 0.25
DN_BETA = (8 * DEPTH) ** -0.25
LN_EPS = 1e-5

SSD_HEAD_DIM = 64
SSD_INNER = D_MODEL
SSD_HEADS = SSD_INNER // SSD_HEAD_DIM
SSD_GROUPS = 2
SSD_STATE = 128
SSD_CONV = 4
SSD_CHUNK = 128

GLA_HEADS = 4
GLA_KEY = D_MODEL // 2
GLA_VAL = D_MODEL
GLA_DK = GLA_KEY // GLA_HEADS
GLA_DV = GLA_VAL // GLA_HEADS
GLA_RANK = 16
GLA_TAU = 16.0
GLA_CHUNK = 64

MIX_WIDTH = SSD_INNER + GLA_VAL

IN_SPLITS = (SSD_INNER, SSD_INNER, SSD_GROUPS * SSD_STATE, SSD_GROUPS * SSD_STATE, SSD_HEADS,
             GLA_KEY, GLA_KEY, GLA_VAL, GLA_VAL, GLA_RANK)
IN_WIDTH = sum(IN_SPLITS)
IN_OFFSETS = tuple(int(o) for o in np.cumsum(IN_SPLITS)[:-1])
CONV_CH = SSD_INNER + 2 * SSD_GROUPS * SSD_STATE

S5_GROUP = 16
S5_GROUPS = D_MODEL // S5_GROUP
S5_STATE = 64

FFN_HIDDEN = -(-8 * D_MODEL // (3 * 256)) * 256

kernel_name = "hybrid_ssd_gla_s5_deepnorm"


def layer_norm(x, g, b):
    xf = x.astype(jnp.float32)
    mu = jnp.mean(xf, axis=-1, keepdims=True)
    var = jnp.mean(jnp.square(xf - mu), axis=-1, keepdims=True)
    return ((xf - mu) * lax.rsqrt(var + LN_EPS) * g + b).astype(x.dtype)


def rms_norm(x, g):
    xf = x.astype(jnp.float32)
    return (xf * lax.rsqrt(jnp.mean(jnp.square(xf), axis=-1, keepdims=True) + LN_EPS) * g).astype(x.dtype)


def causal_depthwise_conv(x, w, b):
    width, c = w.shape
    y = lax.conv_general_dilated(x, w[:, None, :], window_strides=(1,), padding=[(width - 1, 0)],
                                 dimension_numbers=('NWC', 'WIO', 'NWC'), feature_group_count=c)
    return y + b


def ssd_chunked(x, dt, A, Bm, Cm, D_skip):
    b, l, h, p = x.shape
    g, n = Bm.shape[2], Bm.shape[3]
    e = h // g
    q = SSD_CHUNK
    nc = l // q
    x_c = x.reshape(b, nc, q, g, e, p)
    dt_c = dt.reshape(b, nc, q, g, e)
    B_c = Bm.reshape(b, nc, q, g, n)
    C_c = Cm.reshape(b, nc, q, g, n)
    dA_cum = jnp.cumsum(dt_c * A.reshape(g, e), axis=2)
    causal = jnp.tril(jnp.ones((q, q), bool))[:, :, None, None]
    seg = dA_cum[:, :, :, None] - dA_cum[:, :, None, :]
    decay = jnp.exp(jnp.where(causal, seg, -jnp.inf))
    scores = jnp.einsum('bcign,bcjgn->bcijg', C_c, B_c)
    w = scores[..., None] * decay * dt_c[:, :, None]
    y_diag = jnp.einsum('bcijge,bcjgep->bcigep', w, x_c)
    decay_to_end = jnp.exp(dA_cum[:, :, -1:] - dA_cum)
    xw = x_c * (decay_to_end * dt_c)[..., None]
    states = jnp.einsum('bcjgn,bcjgep->bcgepn', B_c, xw)
    chunk_decay = jnp.exp(dA_cum[:, :, -1])

    def step(h_prev, inp):
        dec, st = inp
        return dec[..., None, None] * h_prev + st, h_prev

    h0 = jnp.zeros((b, g, e, p, n), states.dtype)
    _, h_enter = lax.scan(step, h0, (jnp.moveaxis(chunk_decay, 1, 0), jnp.moveaxis(states, 1, 0)))
    h_enter = jnp.moveaxis(h_enter, 0, 1)
    y_off = jnp.einsum('bcign,bcgepn->bcigep', C_c, h_enter) * jnp.exp(dA_cum)[..., None]
    y = y_diag + y_off + D_skip.reshape(g, e)[:, :, None] * x_c
    return y.reshape(b, l, h, p)


def gla_chunked(q, k, v, log_a):
    b, l, h, dk = q.shape
    dv = v.shape[-1]
    c = GLA_CHUNK
    nc = l // c
    causal = jnp.tril(jnp.ones((c, c), bool))[None, :, :, None, None]

    def to_chunks(t):
        return jnp.moveaxis(t.reshape(b, nc, c, h, t.shape[-1]), 1, 0)

    def step(S, inp):
        qc, kc, vc, gc = inp
        G = jnp.cumsum(gc, axis=1)
        rel = jnp.where(causal, G[:, :, None] - G[:, None], -jnp.inf)
        attn = jnp.einsum('bihd,bjhd,bijhd->bhij', qc, kc, jnp.exp(rel))
        o_intra = jnp.einsum('bhij,bjhv->bihv', attn, vc)
        o_inter = jnp.einsum('bihd,bhdv->bihv', qc * jnp.exp(G), S)
        G_last = G[:, -1]
        k_dec = kc * jnp.exp(G_last[:, None] - G)
        S_new = jnp.exp(G_last)[..., None] * S + jnp.einsum('bjhd,bjhv->bhdv', k_dec, vc)
        return S_new, o_intra + o_inter

    S0 = jnp.zeros((b, h, dk, dv), jnp.float32)
    _, o = lax.scan(step, S0, (to_chunks(q), to_chunks(k), to_chunks(v), to_chunks(log_a)))
    return jnp.moveaxis(o, 0, 1).reshape(b, l, h, dv)


def ssd_gla_mixer(x, w_in, conv_w, conv_b, dt_bias, A_log, D_skip, ssd_norm,
                  gla_w_gate, gla_b_gate, gla_norm, w_out):
    b, l, _ = x.shape
    proj = x @ w_in
    z, xs, Bs, Cs, dt_raw, gq, gk, gv, gr, ga = jnp.split(proj, IN_OFFSETS, axis=-1)
    xbc = jax.nn.silu(causal_depthwise_conv(jnp.concatenate([xs, Bs, Cs], axis=-1), conv_w, conv_b))
    xs, Bs, Cs = jnp.split(xbc, (SSD_INNER, SSD_INNER + SSD_GROUPS * SSD_STATE), axis=-1)
    dt = jax.nn.softplus((dt_raw + dt_bias).astype(jnp.float32))
    A = -jnp.exp(A_log.astype(jnp.float32))
    y_ssd = ssd_chunked(xs.reshape(b, l, SSD_HEADS, SSD_HEAD_DIM), dt, A,
                        Bs.reshape(b, l, SSD_GROUPS, SSD_STATE), Cs.reshape(b, l, SSD_GROUPS, SSD_STATE), D_skip)
    y_ssd = rms_norm(y_ssd.reshape(b, l, SSD_INNER) * jax.nn.silu(z), ssd_norm)
    q = gq.reshape(b, l, GLA_HEADS, GLA_DK) * GLA_DK ** -0.5
    k = gk.reshape(b, l, GLA_HEADS, GLA_DK)
    v = gv.reshape(b, l, GLA_HEADS, GLA_DV)
    log_a = jax.nn.log_sigmoid((ga @ gla_w_gate + gla_b_gate).astype(jnp.float32)) / GLA_TAU
    o = gla_chunked(q, k, v, log_a.reshape(b, l, GLA_HEADS, GLA_DK))
    o = rms_norm(o, gla_norm.reshape(GLA_HEADS, GLA_DV))
    y_gla = o.reshape(b, l, GLA_VAL) * jax.nn.silu(gr)
    y = jnp.concatenate([y_ssd.astype(x.dtype), y_gla.astype(x.dtype)], axis=-1)
    return y @ w_out


def s5_mixer(u, A_re, A_im, log_step, B_re, B_im, C_re, C_im, D_skip, glu_w_a, glu_b_a, glu_w_b, glu_b_b):
    b, l, d = u.shape
    A_re = A_re.astype(jnp.float32)
    A_im = A_im.astype(jnp.float32)
    step = jnp.exp(log_step.astype(jnp.float32))[:, None]
    mag = jnp.exp(A_re * step)
    ab_re = mag * jnp.cos(A_im * step)
    ab_im = mag * jnp.sin(A_im * step)
    num_re = ab_re - 1.0
    den = jnp.square(A_re) + jnp.square(A_im)
    f_re = (num_re * A_re + ab_im * A_im) / den
    f_im = (ab_im * A_re - num_re * A_im) / den
    Bb_re = f_re[..., None] * B_re - f_im[..., None] * B_im
    Bb_im = f_re[..., None] * B_im + f_im[..., None] * B_re
    ug = u.reshape(b, l, S5_GROUPS, S5_GROUP)
    bu_re = jnp.einsum('gpc,blgc->blgp', Bb_re, ug)
    bu_im = jnp.einsum('gpc,blgc->blgp', Bb_im, ug)
    a_re = jnp.broadcast_to(ab_re, (1, l) + ab_re.shape)
    a_im = jnp.broadcast_to(ab_im, (1, l) + ab_im.shape)

    def combine(e1, e2):
        a1r, a1i, b1r, b1i = e1
        a2r, a2i, b2r, b2i = e2
        return (a2r * a1r - a2i * a1i, a2r * a1i + a2i * a1r,
                a2r * b1r - a2i * b1i + b2r, a2r * b1i + a2i * b1r + b2i)

    _, _, s_re, s_im = lax.associative_scan(combine, (a_re, a_im, bu_re, bu_im), axis=1)
    y = jnp.einsum('gcp,blgp->blgc', C_re, s_re) - jnp.einsum('gcp,blgp->blgc', C_im, s_im)
    y = y.reshape(b, l, d).astype(u.dtype) + D_skip * u
    y = jax.nn.gelu(y)
    return (y @ glu_w_a + glu_b_a) * jax.nn.sigmoid(y @ glu_w_b + glu_b_b)


def swiglu_ffn(x, w_gate, w_up, w_down):
    return (jax.nn.silu(x @ w_gate) * (x @ w_up)) @ w_down


def setup_inputs(seed: int = 0) -> dict:
    key = jax.random.key(seed)
    ks = iter(jax.random.split(key, 64))

    def nrm(shape, scale):
        return scale * jax.random.normal(next(ks), shape, jnp.float32)

    def gain(n):
        return 1.0 + nrm((n,), 0.02)

    def bias(n):
        return nrm((n,), 0.02)

    inp = {}
    inp['x'] = jax.random.normal(next(ks), (BATCH, SEQ, D_MODEL), jnp.float32)
    inp['l0_w_in'] = nrm((D_MODEL, IN_WIDTH), D_MODEL ** -0.5)
    inp['l0_conv_w'] = nrm((SSD_CONV, CONV_CH), SSD_CONV ** -0.5)
    inp['l0_conv_b'] = bias(CONV_CH)
    dt0 = jnp.exp(jax.random.uniform(next(ks), (SSD_HEADS,), jnp.float32, math.log(1e-3), math.log(1e-1)))
    inp['l0_dt_bias'] = dt0 + jnp.log(-jnp.expm1(-dt0))
    inp['l0_A_log'] = jnp.log(jax.random.uniform(next(ks), (SSD_HEADS,), jnp.float32, 1.0, 16.0))
    inp['l0_D'] = 1.0 + nrm((SSD_HEADS,), 0.1)
    inp['l0_ssd_norm'] = gain(SSD_INNER)
    inp['l0_gla_w_gate'] = nrm((GLA_RANK, GLA_KEY), GLA_RANK ** -0.5)
    inp['l0_gla_b_gate'] = bias(GLA_KEY)
    inp['l0_gla_norm'] = gain(GLA_VAL)
    inp['l0_w_out'] = nrm((MIX_WIDTH, D_MODEL), DN_BETA * MIX_WIDTH ** -0.5)
    inp['l0_ln1_g'] = gain(D_MODEL)
    inp['l0_ln1_b'] = bias(D_MODEL)
    inp['l0_ffn_w_gate'] = nrm((D_MODEL, FFN_HIDDEN), D_MODEL ** -0.5)
    inp['l0_ffn_w_up'] = nrm((D_MODEL, FFN_HIDDEN), D_MODEL ** -0.5)
    inp['l0_ffn_w_down'] = nrm((FFN_HIDDEN, D_MODEL), DN_BETA * FFN_HIDDEN ** -0.5)
    inp['l0_ln2_g'] = gain(D_MODEL)
    inp['l0_ln2_b'] = bias(D_MODEL)
    inp['l1_s5_A_re'] = -0.5 + nrm((S5_GROUPS, S5_STATE), 0.01)
    inp['l1_s5_A_im'] = jnp.broadcast_to(jnp.pi * jnp.arange(S5_STATE, dtype=jnp.float32), (S5_GROUPS, S5_STATE)) + nrm((S5_GROUPS, S5_STATE), 0.01)
    inp['l1_s5_log_step'] = jax.random.uniform(next(ks), (S5_GROUPS,), jnp.float32, math.log(1e-3), math.log(1e-1))
    inp['l1_s5_B_re'] = nrm((S5_GROUPS, S5_STATE, S5_GROUP), (2 * S5_GROUP) ** -0.5)
    inp['l1_s5_B_im'] = nrm((S5_GROUPS, S5_STATE, S5_GROUP), (2 * S5_GROUP) ** -0.5)
    inp['l1_s5_C_re'] = nrm((S5_GROUPS, S5_GROUP, S5_STATE), (2 * S5_STATE) ** -0.5 * 4.0)
    inp['l1_s5_C_im'] = nrm((S5_GROUPS, S5_GROUP, S5_STATE), (2 * S5_STATE) ** -0.5 * 4.0)
    inp['l1_s5_D'] = nrm((D_MODEL,), 0.5)
    inp['l1_glu_w_a'] = nrm((D_MODEL, D_MODEL), DN_BETA * D_MODEL ** -0.5)
    inp['l1_glu_b_a'] = bias(D_MODEL)
    inp['l1_glu_w_b'] = nrm((D_MODEL, D_MODEL), D_MODEL ** -0.5)
    inp['l1_glu_b_b'] = bias(D_MODEL)
    inp['l1_ln1_g'] = gain(D_MODEL)
    inp['l1_ln1_b'] = bias(D_MODEL)
    inp['l1_ffn_w_gate'] = nrm((D_MODEL, FFN_HIDDEN), D_MODEL ** -0.5)
    inp['l1_ffn_w_up'] = nrm((D_MODEL, FFN_HIDDEN), D_MODEL ** -0.5)
    inp['l1_ffn_w_down'] = nrm((FFN_HIDDEN, D_MODEL), DN_BETA * FFN_HIDDEN ** -0.5)
    inp['l1_ln2_g'] = gain(D_MODEL)
    inp['l1_ln2_b'] = bias(D_MODEL)
    return inp


def reference(x, l0_w_in, l0_conv_w, l0_conv_b, l0_dt_bias, l0_A_log, l0_D, l0_ssd_norm,
              l0_gla_w_gate, l0_gla_b_gate, l0_gla_norm, l0_w_out, l0_ln1_g, l0_ln1_b,
              l0_ffn_w_gate, l0_ffn_w_up, l0_ffn_w_down, l0_ln2_g, l0_ln2_b,
              l1_s5_A_re, l1_s5_A_im, l1_s5_log_step, l1_s5_B_re, l1_s5_B_im, l1_s5_C_re, l1_s5_C_im,
              l1_s5_D, l1_glu_w_a, l1_glu_b_a, l1_glu_w_b, l1_glu_b_b, l1_ln1_g, l1_ln1_b,
              l1_ffn_w_gate, l1_ffn_w_up, l1_ffn_w_down, l1_ln2_g, l1_ln2_b):
    mixer_params = [
        (l0_w_in, l0_conv_w, l0_conv_b, l0_dt_bias, l0_A_log, l0_D, l0_ssd_norm,
         l0_gla_w_gate, l0_gla_b_gate, l0_gla_norm, l0_w_out),
        (l1_s5_A_re, l1_s5_A_im, l1_s5_log_step, l1_s5_B_re, l1_s5_B_im, l1_s5_C_re, l1_s5_C_im,
         l1_s5_D, l1_glu_w_a, l1_glu_b_a, l1_glu_w_b, l1_glu_b_b),
    ]
    norm1 = [(l0_ln1_g, l0_ln1_b), (l1_ln1_g, l1_ln1_b)]
    ffn_params = [(l0_ffn_w_gate, l0_ffn_w_up, l0_ffn_w_down), (l1_ffn_w_gate, l1_ffn_w_up, l1_ffn_w_down)]
    norm2 = [(l0_ln2_g, l0_ln2_b), (l1_ln2_g, l1_ln2_b)]
    for layer in range(DEPTH):
        if layer % 2 == 0:
            h = ssd_gla_mixer(x, *mixer_params[layer])
        else:
            h = s5_mixer(x, *mixer_params[layer])
        x = layer_norm(DN_ALPHA * x + h, *norm1[layer])
        x = layer_norm(DN_ALPHA * x + swiglu_ffn(x, *ffn_params[layer]), *norm2[layer])
    return x
```

```python
import functools

import jax
import jax.numpy as jnp
from jax import lax
from jax.experimental import pallas as pl
from jax.experimental.pallas import tpu as pltpu

F32 = jnp.float32
BF16 = jnp.bfloat16
HI = lax.Precision.HIGHEST

D_MODEL = 1024
DEPTH = 2
DN_ALPHA = (2 * DEPTH) ** 0.25
LN_EPS = 1e-5

SSD_HEADS = 16
SSD_HEAD_DIM = 64
SSD_GROUPS = 2
SSD_STATE = 128
SSD_INNER = 1024
SSD_CONV = 4
SSD_CHUNK = 128
CONV_CH = 1536
CONV_TAIL = 8

GLA_HEADS = 4
GLA_KEY = 512
GLA_VAL = 1024
GLA_DK = 128
GLA_DV = 256
GLA_RANK = 16
GLA_TAU = 16.0
GLA_CHUNK = 64

S5_GROUP = 16
S5_GROUPS = 64
S5_STATE = 64
S5_CHUNK = 16
S5_GB = 8

FFN_HIDDEN = 2816

TM = 512
VMEM_LIMIT = 56 * 1024 * 1024

NT = (((1,), (1,)), ((), ()))
TN = (((0,), (0,)), ((), ()))


def _const_spec(shape):
    nd = len(shape)
    return pl.BlockSpec(shape, lambda *_: (0,) * nd, pipeline_mode=pl.Buffered(1))


def _params(sem):
    return pltpu.CompilerParams(dimension_semantics=sem, vmem_limit_bytes=VMEM_LIMIT)


def _layer_norm(x, g, b):
    mu = jnp.mean(x, axis=-1, keepdims=True)
    xc = x - mu
    var = jnp.mean(xc * xc, axis=-1, keepdims=True)
    return xc * lax.rsqrt(var + LN_EPS) * g + b


def _silu(x):
    return x * jax.nn.sigmoid(x)


_IN_SEGS = ((0, 1024), (1024, 1536), (2560, 512), (3072, 512), (3584, 1024), (4608, 1024))
_IN_MAIN = 5632


def _in_proj_kernel(x_ref, wm_ref, ws_ref, z_ref, xbc_ref, q_ref, k_ref, v_ref, r_ref, sm_ref, smt_ref):
    x = x_ref[...]
    xb = x.astype(BF16)
    for ref, (start, width) in zip((z_ref, xbc_ref, q_ref, k_ref, v_ref, r_ref), _IN_SEGS):
        for c in range(0, width, 512):
            ref[:, c:c + 512] = jnp.dot(xb, wm_ref[:, start + c:start + c + 512],
                                        preferred_element_type=F32).astype(ref.dtype)
    sm = jnp.dot(x, ws_ref[...], precision=HI, preferred_element_type=F32)
    sm_ref[...] = sm
    smt_ref[...] = sm.T[0:SSD_HEADS, :]


def _in_proj(x2, w_main, w_small):
    t = x2.shape[0]
    tok = lambda w: pl.BlockSpec((TM, w), lambda i: (i, 0))
    outs = [jax.ShapeDtypeStruct((t, w), BF16) for _, w in _IN_SEGS]
    outs += [jax.ShapeDtypeStruct((t, 128), F32), jax.ShapeDtypeStruct((SSD_HEADS, t), F32)]
    out_specs = [tok(w) for _, w in _IN_SEGS]
    out_specs += [tok(128), pl.BlockSpec((SSD_HEADS, TM), lambda i: (0, i))]
    return pl.pallas_call(
        _in_proj_kernel,
        grid=(t // TM,),
        in_specs=[tok(D_MODEL), _const_spec(w_main.shape), _const_spec(w_small.shape)],
        out_specs=out_specs,
        out_shape=outs,
        compiler_params=_params(("arbitrary",)),
        name="in_proj",
    )(x2, w_main, w_small)


def _ssd_kernel(z_ref, xbc_ref, sm_ref, smt_ref, cw_ref, cb_ref, dtbr_ref, dtbc_ref, alr_ref, alc_ref,
                drow_ref, g_ref, out_ref, xpad, state, y_scr, xw_scr):
    q = SSD_CHUNK

    @pl.when(pl.program_id(1) == 0)
    def _():
        xpad[0:CONV_TAIL, :] = jnp.zeros((CONV_TAIL, CONV_CH), F32)
        state[...] = jnp.zeros(state.shape, F32)

    cur = xbc_ref[...].astype(F32)
    xpad[CONV_TAIL:CONV_TAIL + q, :] = cur
    acc = cb_ref[...]
    for k in range(SSD_CONV):
        off = CONV_TAIL - (SSD_CONV - 1) + k
        acc = acc + cw_ref[k:k + 1, :] * xpad[off:off + q, :]
    xpad[0:CONV_TAIL, :] = cur[q - CONV_TAIL:q, :]
    xbc = _silu(acc)
    xs = xbc[:, 0:SSD_INNER]

    dt = jax.nn.softplus(sm_ref[:, 0:SSD_HEADS] + dtbr_ref[...])
    dt_t = jax.nn.softplus(smt_ref[...] + dtbc_ref[...])
    a_row = -jnp.exp(alr_ref[...])
    a_col = -jnp.exp(alc_ref[...])
    ii = lax.broadcasted_iota(jnp.int32, (q, q), 0)
    jj = lax.broadcasted_iota(jnp.int32, (q, q), 1)
    causal = ii >= jj
    cum = jnp.dot(causal.astype(F32), dt * a_row, precision=HI, preferred_element_type=F32)
    cum_t = jnp.dot(dt_t * a_col, (ii <= jj).astype(F32), precision=HI, preferred_element_type=F32)
    cum_last = cum[q - 1:q, :]
    to_end = jnp.exp(cum_last - cum) * dt
    ecum = jnp.exp(cum)
    chunk_decay = jnp.exp(cum_last)

    hpg = SSD_HEADS // SSD_GROUPS
    gw = hpg * SSD_HEAD_DIM
    for g in range(SSD_GROUPS):
        bg = xbc[:, SSD_INNER + g * SSD_STATE:SSD_INNER + (g + 1) * SSD_STATE].astype(BF16)
        cg = xbc[:, SSD_INNER + (SSD_GROUPS + g) * SSD_STATE:
                 SSD_INNER + (SSD_GROUPS + g + 1) * SSD_STATE].astype(BF16)
        scores = lax.dot_general(cg, bg, NT, preferred_element_type=F32)
        h_enter = state[g]
        y_off = jnp.dot(cg, h_enter.astype(BF16), preferred_element_type=F32)
        dec_cols = []
        for hh in range(hpg):
            h = g * hpg + hh
            lo = h * SSD_HEAD_DIM
            seg = jnp.where(causal, cum[:, h:h + 1] - cum_t[h:h + 1, :], -1e30)
            w = (scores * jnp.exp(seg) * dt_t[h:h + 1, :]).astype(BF16)
            xh = xs[:, lo:lo + SSD_HEAD_DIM]
            y_h = jnp.dot(w, xh.astype(BF16), preferred_element_type=F32)
            y_h = y_h + y_off[:, hh * SSD_HEAD_DIM:(hh + 1) * SSD_HEAD_DIM] * ecum[:, h:h + 1]
            y_scr[:, lo:lo + SSD_HEAD_DIM] = y_h
            xw_scr[:, lo:lo + SSD_HEAD_DIM] = (xh * to_end[:, h:h + 1]).astype(BF16)
            dec_cols.append(jnp.broadcast_to(chunk_decay[:, h:h + 1], (1, SSD_HEAD_DIM)))
        upd = lax.dot_general(bg, xw_scr[:, g * gw:(g + 1) * gw], TN, preferred_element_type=F32)
        state[g] = h_enter * jnp.concatenate(dec_cols, axis=1) + upd

    y = y_scr[...] + drow_ref[...] * xs
    zf = z_ref[...].astype(F32)
    y = y * _silu(zf)
    ms = jnp.mean(y * y, axis=-1, keepdims=True)
    out_ref[...] = (y * lax.rsqrt(ms + LN_EPS) * g_ref[...]).astype(out_ref.dtype)


def _ssd(z, xbc, sm, smt, conv_w, conv_b, dt_bias, a_log, d_skip, norm_g, batch):
    t = z.shape[0]
    nc = t // batch // SSD_CHUNK
    q = SSD_CHUNK
    tok = lambda w: pl.BlockSpec((q, w), lambda b, c: (b * nc + c, 0))
    hpg = SSD_HEADS // SSD_GROUPS
    args = (z, xbc, sm, smt, conv_w, conv_b.reshape(1, -1), dt_bias.reshape(1, -1), dt_bias.reshape(-1, 1),
            a_log.reshape(1, -1), a_log.reshape(-1, 1), jnp.repeat(d_skip, SSD_HEAD_DIM).reshape(1, -1),
            norm_g.reshape(1, -1))
    in_specs = [tok(SSD_INNER), tok(CONV_CH), tok(128), pl.BlockSpec((SSD_HEADS, q), lambda b, c: (0, b * nc + c))]
    in_specs += [_const_spec(a.shape) for a in args[4:]]
    return pl.pallas_call(
        _ssd_kernel,
        grid=(batch, nc),
        in_specs=in_specs,
        out_specs=tok(SSD_INNER),
        out_shape=jax.ShapeDtypeStruct((t, SSD_INNER), BF16),
        scratch_shapes=[pltpu.VMEM((CONV_TAIL + q, CONV_CH), F32),
                        pltpu.VMEM((SSD_GROUPS, SSD_STATE, hpg * SSD_HEAD_DIM), F32),
                        pltpu.VMEM((q, SSD_INNER), F32),
                        pltpu.VMEM((q, SSD_INNER), BF16)],
        compiler_params=_params(("arbitrary", "arbitrary")),
        name="ssd",
    )(*args)


def _gla_kernel(q_ref, k_ref, v_ref, r_ref, sm_ref, wg_ref, bg_ref, g_ref, out_ref, state_t):
    c = GLA_CHUNK

    @pl.when(pl.program_id(1) == 0)
    def _():
        state_t[...] = jnp.zeros(state_t.shape, F32)

    ga = sm_ref[:, SSD_HEADS:SSD_HEADS + GLA_RANK]
    pre = jnp.dot(ga, wg_ref[...], precision=HI, preferred_element_type=F32) + bg_ref[...]
    log_a = jax.nn.log_sigmoid(pre) / GLA_TAU
    ii = lax.broadcasted_iota(jnp.int32, (c, c), 0)
    jj = lax.broadcasted_iota(jnp.int32, (c, c), 1)
    causal = ii >= jj
    gcum = jnp.dot(causal.astype(F32), log_a, precision=HI, preferred_element_type=F32)
    g_mid = gcum[c // 2 - 1:c // 2, :]
    g_last = gcum[c - 1:c, :]
    qf = q_ref[...].astype(F32) * (GLA_DK ** -0.5)
    kf = k_ref[...].astype(F32)
    q_in = (qf * jnp.exp(gcum - g_mid)).astype(BF16)
    k_in = (kf * jnp.exp(g_mid - gcum)).astype(BF16)
    q_st = (qf * jnp.exp(gcum)).astype(BF16)
    k_dec = (kf * jnp.exp(g_last - gcum)).astype(BF16)
    dec_last = jnp.exp(g_last)

    for h in range(GLA_HEADS):
        ks = slice(h * GLA_DK, (h + 1) * GLA_DK)
        vs = slice(h * GLA_DV, (h + 1) * GLA_DV)
        attn = lax.dot_general(q_in[:, ks], k_in[:, ks], NT, preferred_element_type=F32)
        attn = jnp.where(causal, attn, 0.0).astype(BF16)
        vh = v_ref[:, vs]
        s_t = state_t[h]
        o = jnp.dot(attn, vh, preferred_element_type=F32)
        o = o + lax.dot_general(q_st[:, ks], s_t.astype(BF16), NT, preferred_element_type=F32)
        state_t[h] = s_t * dec_last[:, ks] + lax.dot_general(vh, k_dec[:, ks], TN, preferred_element_type=F32)
        ms = jnp.mean(o * o, axis=-1, keepdims=True)
        o = o * lax.rsqrt(ms + LN_EPS) * g_ref[:, vs]
        rf = r_ref[:, vs].astype(F32)
        out_ref[:, vs] = (o * _silu(rf)).astype(out_ref.dtype)


def _gla(q, k, v, r, sm, w_gate, b_gate, norm_g, batch):
    t = q.shape[0]
    c = GLA_CHUNK
    nc = t // batch // c
    tok = lambda w: pl.BlockSpec((c, w), lambda b, i: (b * nc + i, 0))
    args = (q, k, v, r, sm, w_gate, b_gate.reshape(1, -1), norm_g.reshape(1, -1))
    in_specs = [tok(GLA_KEY), tok(GLA_KEY), tok(GLA_VAL), tok(GLA_VAL), tok(128)]
    in_specs += [_const_spec(a.shape) for a in args[5:]]
    return pl.pallas_call(
        _gla_kernel,
        grid=(batch, nc),
        in_specs=in_specs,
        out_specs=tok(GLA_VAL),
        out_shape=jax.ShapeDtypeStruct((t, GLA_VAL), BF16),
        scratch_shapes=[pltpu.VMEM((GLA_HEADS, GLA_DV, GLA_DK), F32)],
        compiler_params=_params(("arbitrary", "arbitrary")),
        name="gla",
    )(*args)


def _out_proj_ln_kernel(x_ref, ys_ref, yg_ref, w_ref, g_ref, b_ref, o_ref):
    h = jnp.dot(ys_ref[...], w_ref[0:SSD_INNER, :], preferred_element_type=F32)
    h = h + jnp.dot(yg_ref[...], w_ref[SSD_INNER:SSD_INNER + GLA_VAL, :], preferred_element_type=F32)
    o_ref[...] = _layer_norm(DN_ALPHA * x_ref[...] + h, g_ref[...], b_ref[...])


def _out_proj_ln(x2, y_ssd, y_gla, w_out, ln_g, ln_b):
    t = x2.shape[0]
    tok = lambda w: pl.BlockSpec((TM, w), lambda i: (i, 0))
    args = (x2, y_ssd, y_gla, w_out, ln_g.reshape(1, -1), ln_b.reshape(1, -1))
    return pl.pallas_call(
        _out_proj_ln_kernel,
        grid=(t // TM,),
        in_specs=[tok(D_MODEL), tok(SSD_INNER), tok(GLA_VAL)] + [_const_spec(a.shape) for a in args[3:]],
        out_specs=tok(D_MODEL),
        out_shape=jax.ShapeDtypeStruct((t, D_MODEL), F32),
        compiler_params=_params(("arbitrary",)),
        name="out_proj_ln",
    )(*args)


_FFN_CHUNKS = tuple((s, min(512, FFN_HIDDEN - s)) for s in range(0, FFN_HIDDEN, 512))


def _ffn_ln_kernel(x_ref, wg_ref, wu_ref, wd_ref, g_ref, b_ref, o_ref):
    x = x_ref[...]
    xb = x.astype(BF16)
    acc = jnp.zeros((TM, D_MODEL), F32)
    for s, w in _FFN_CHUNKS:
        gate = jnp.dot(xb, wg_ref[:, s:s + w], preferred_element_type=F32)
        up = jnp.dot(xb, wu_ref[:, s:s + w], preferred_element_type=F32)
        hid = (_silu(gate) * up).astype(BF16)
        acc = acc + jnp.dot(hid, wd_ref[s:s + w, :], preferred_element_type=F32)
    o_ref[...] = _layer_norm(DN_ALPHA * x + acc, g_ref[...], b_ref[...])


def _ffn_ln(x2, w_gate, w_up, w_down, ln_g, ln_b):
    t = x2.shape[0]
    tok = pl.BlockSpec((TM, D_MODEL), lambda i: (i, 0))
    args = (x2, w_gate, w_up, w_down, ln_g.reshape(1, -1), ln_b.reshape(1, -1))
    return pl.pallas_call(
        _ffn_ln_kernel,
        grid=(t // TM,),
        in_specs=[tok] + [_const_spec(a.shape) for a in args[1:]],
        out_specs=tok,
        out_shape=jax.ShapeDtypeStruct((t, D_MODEL), F32),
        compiler_params=_params(("arbitrary",)),
        name="ffn_ln",
    )(*args)


def _glu_ln_kernel(u_ref, ys_ref, d_ref, wa_ref, ba_ref, wb_ref, bb_ref, g_ref, b_ref, o_ref):
    u = u_ref[...]
    y = ys_ref[...].astype(F32) + d_ref[...] * u
    yb = jax.nn.gelu(y).astype(BF16)
    a = jnp.dot(yb, wa_ref[...], preferred_element_type=F32) + ba_ref[...]
    b = jnp.dot(yb, wb_ref[...], preferred_element_type=F32) + bb_ref[...]
    o_ref[...] = _layer_norm(DN_ALPHA * u + a * jax.nn.sigmoid(b), g_ref[...], b_ref[...])


def _glu_ln(u2, y_s5, d_skip, w_a, b_a, w_b, b_b, ln_g, ln_b):
    t = u2.shape[0]
    tok = pl.BlockSpec((TM, D_MODEL), lambda i: (i, 0))
    row = lambda a: a.reshape(1, -1)
    args = (u2, y_s5, row(d_skip), w_a, row(b_a), w_b, row(b_b), row(ln_g), row(ln_b))
    return pl.pallas_call(
        _glu_ln_kernel,
        grid=(t // TM,),
        in_specs=[tok, tok] + [_const_spec(a.shape) for a in args[2:]],
        out_specs=tok,
        out_shape=jax.ShapeDtypeStruct((t, D_MODEL), F32),
        compiler_params=_params(("arbitrary",)),
        name="glu_ln",
    )(*args)


def _s5_weights_kernel(are_c, aim_c, st_c, are_r, aim_r, st_r, ctr_ref, cti_ref, btr_ref, bti_ref,
                       wt_ref, ws_ref, wc_ref):
    q = S5_CHUNK
    n = q * S5_GROUP
    a = are_c[...] * st_c[...]
    th = aim_c[...] * st_c[...]
    kk = (lax.broadcasted_iota(jnp.int32, (1, 1, n), 2) // S5_GROUP).astype(F32)
    mag = jnp.exp(kk * a)
    p_re = mag * jnp.cos(kk * th)
    p_im = mag * jnp.sin(kk * th)
    l_re = jnp.exp(a) * jnp.cos(th)
    l_im = jnp.exp(a) * jnp.sin(th)
    p1_re = p_re * l_re - p_im * l_im
    p1_im = p_re * l_im + p_im * l_re
    c_re = ctr_ref[...]
    c_im = cti_ref[...]
    cp = jnp.concatenate([c_re * p_re - c_im * p_im, -(c_re * p_im + c_im * p_re)], axis=1)
    wc = jnp.concatenate([c_re * p1_re - c_im * p1_im, -(c_re * p1_im + c_im * p1_re)], axis=1)
    wc_ref[...] = wc.astype(wc_ref.dtype)

    ar = are_r[...]
    ai = aim_r[...]
    a2 = ar * st_r[...]
    th2 = ai * st_r[...]
    ab_re = jnp.exp(a2) * jnp.cos(th2)
    ab_im = jnp.exp(a2) * jnp.sin(th2)
    num_re = ab_re - 1.0
    den = ar * ar + ai * ai
    f_re = (num_re * ar + ab_im * ai) / den
    f_im = (ab_im * ar - num_re * ai) / den
    b_re = btr_ref[...]
    b_im = bti_ref[...]
    bb_re = f_re * b_re - f_im * b_im
    bb_im = f_re * b_im + f_im * b_re

    k2 = (q - 1 - lax.broadcasted_iota(jnp.int32, (1, q, 1), 1)).astype(F32)
    mag2 = jnp.exp(k2 * a2)
    e_re = mag2 * jnp.cos(k2 * th2)
    e_im = mag2 * jnp.sin(k2 * th2)
    for j in range(q):
        pr = e_re[:, j:j + 1, :]
        pi = e_im[:, j:j + 1, :]
        rows = slice(j * S5_GROUP, (j + 1) * S5_GROUP)
        ws_ref[:, rows, 0:S5_STATE] = (bb_re * pr - bb_im * pi).astype(ws_ref.dtype)
        ws_ref[:, rows, S5_STATE:2 * S5_STATE] = (bb_re * pi + bb_im * pr).astype(ws_ref.dtype)

    lane = lax.broadcasted_iota(jnp.int32, (S5_GROUP, n), 1)
    for g in range(S5_GB):
        bcat = jnp.concatenate([bb_re[g], bb_im[g]], axis=1)
        lag = jnp.dot(bcat, cp[g], precision=HI, preferred_element_type=F32)
        for j in range(q):
            sh = j * S5_GROUP
            blk = lag if j == 0 else jnp.where(lane >= sh, pltpu.roll(lag, sh, 1), 0.0)
            wt_ref[g, j * S5_GROUP:(j + 1) * S5_GROUP, :] = blk.astype(wt_ref.dtype)


def _s5_weights(a_re, a_im, log_step, b_re, b_im, c_re, c_im):
    g, p = a_re.shape
    n = S5_CHUNK * S5_GROUP
    step = jnp.exp(log_step.astype(F32))
    tile_c = lambda c: jnp.tile(jnp.transpose(c, (0, 2, 1)), (1, 1, S5_CHUNK))
    args = (a_re.reshape(g, p, 1), a_im.reshape(g, p, 1), step.reshape(g, 1, 1),
            a_re.reshape(g, 1, p), a_im.reshape(g, 1, p), step.reshape(g, 1, 1),
            tile_c(c_re), tile_c(c_im), jnp.transpose(b_re, (0, 2, 1)), jnp.transpose(b_im, (0, 2, 1)))
    blk = lambda a: pl.BlockSpec((S5_GB,) + a.shape[1:], lambda i: (i, 0, 0))
    out_shapes = [jax.ShapeDtypeStruct((g, n, n), BF16), jax.ShapeDtypeStruct((g, n, 2 * p), BF16),
                  jax.ShapeDtypeStruct((g, 2 * p, n), BF16)]
    return pl.pallas_call(
        _s5_weights_kernel,
        grid=(g // S5_GB,),
        in_specs=[blk(a) for a in args],
        out_specs=[blk(s) for s in out_shapes],
        out_shape=out_shapes,
        compiler_params=_params(("arbitrary",)),
        name="s5_weights",
    )(*args)


def _s5_state_kernel(x_ref, ws_ref, o_ref):
    for g in range(S5_GB):
        o_ref[:, g * 128:(g + 1) * 128] = jnp.dot(x_ref[g], ws_ref[g], preferred_element_type=F32)


def _s5_scan_kernel(sl_ref, are_ref, aim_ref, st_ref, o_ref, s, s_sw, *, steps):
    @pl.when(pl.program_id(1) == 0)
    def _():
        s[...] = jnp.zeros(s.shape, F32)
        s_sw[...] = jnp.zeros(s.shape, F32)

    a = are_ref[...] * st_ref[...] * float(S5_CHUNK)
    th = aim_ref[...] * st_ref[...] * float(S5_CHUNK)
    t_re = jnp.exp(a) * jnp.cos(th)
    t_im = jnp.exp(a) * jnp.sin(th)
    lane = lax.broadcasted_iota(jnp.int32, t_im.shape, 1)
    t_x = jnp.where(lane < S5_STATE, -t_im, t_im)

    def body(c, carry):
        cur, cur_sw = carry
        o_ref[c] = cur
        x = sl_ref[c]
        x_sw = pltpu.roll(x, S5_STATE, 1)
        return t_re * cur + t_x * cur_sw + x, t_re * cur_sw - t_x * cur + x_sw

    cur, cur_sw = lax.fori_loop(0, steps, body, (s[...], s_sw[...]))
    s[...] = cur
    s_sw[...] = cur_sw


def _s5_out_kernel(x_ref, se_ref, wt_ref, wc_ref, o_ref):
    for g in range(S5_GB):
        y = jnp.dot(x_ref[g], wt_ref[g], preferred_element_type=F32)
        y = y + jnp.dot(se_ref[:, g * 128:(g + 1) * 128].astype(BF16), wc_ref[g], preferred_element_type=F32)
        o_ref[g] = y.astype(o_ref.dtype)


def _s5(u2, a_re, a_im, log_step, b_re, b_im, c_re, c_im, batch):
    t = u2.shape[0]
    g, p = a_re.shape
    q = S5_CHUNK
    n = q * S5_GROUP
    nc = t // q
    ncb = nc // batch
    wt, ws, wc = _s5_weights(a_re, a_im, log_step, b_re, b_im, c_re, c_im)
    xg = jnp.transpose(u2.astype(BF16).reshape(nc, q, g, S5_GROUP), (2, 0, 1, 3)).reshape(g, nc, n)

    grp = lambda *tail: pl.BlockSpec((S5_GB,) + tail, lambda i: (i, 0, 0))
    s_loc = pl.pallas_call(
        _s5_state_kernel,
        grid=(g // S5_GB,),
        in_specs=[grp(nc, n), grp(n, 2 * p)],
        out_specs=pl.BlockSpec((nc, S5_GB * 2 * p), lambda i: (0, i)),
        out_shape=jax.ShapeDtypeStruct((nc, g * 2 * p), F32),
        compiler_params=_params(("arbitrary",)),
        name="s5_state",
    )(xg, ws)

    steps = min(ncb, 64)
    step = jnp.exp(log_step.astype(F32))
    twice = lambda a: jnp.concatenate([a, a], axis=1)
    scan_args = (s_loc.reshape(nc, g, 2 * p), twice(a_re), twice(a_im), jnp.broadcast_to(step[:, None], (g, 2 * p)))
    chunk_blk = pl.BlockSpec((steps, g, 2 * p), lambda b, i: (b * (ncb // steps) + i, 0, 0))
    s_enter = pl.pallas_call(
        functools.partial(_s5_scan_kernel, steps=steps),
        grid=(batch, ncb // steps),
        in_specs=[chunk_blk] + [_const_spec(a.shape) for a in scan_args[1:]],
        out_specs=chunk_blk,
        out_shape=jax.ShapeDtypeStruct((nc, g, 2 * p), F32),
        scratch_shapes=[pltpu.VMEM((g, 2 * p), F32), pltpu.VMEM((g, 2 * p), F32)],
        compiler_params=_params(("arbitrary", "arbitrary")),
        name="s5_scan",
    )(*scan_args)

    yg = pl.pallas_call(
        _s5_out_kernel,
        grid=(g // S5_GB,),
        in_specs=[grp(nc, n), pl.BlockSpec((nc, S5_GB * 2 * p), lambda i: (0, i)), grp(n, n), grp(2 * p, n)],
        out_specs=grp(nc, n),
        out_shape=jax.ShapeDtypeStruct((g, nc, n), BF16),
        compiler_params=_params(("arbitrary",)),
        name="s5_out",
    )(xg, s_enter.reshape(nc, g * 2 * p), wt, wc)
    return jnp.transpose(yg.reshape(g, nc, q, S5_GROUP), (1, 2, 0, 3)).reshape(t, g * S5_GROUP)


def _layer0(x2, batch, w_in, conv_w, conv_b, dt_bias, a_log, d_skip, ssd_norm, gla_w_gate, gla_b_gate,
            gla_norm, w_out, ln1_g, ln1_b):
    w_main = jnp.concatenate([w_in[:, 0:2560], w_in[:, 2576:5648]], axis=1).astype(BF16)
    w_small = jnp.concatenate([w_in[:, 2560:2576], w_in[:, 5648:5664],
                               jnp.zeros((D_MODEL, 128 - SSD_HEADS - GLA_RANK), F32)], axis=1)
    z, xbc, q, k, v, r, sm, smt = _in_proj(x2, w_main, w_small)
    y_ssd = _ssd(z, xbc, sm, smt, conv_w, conv_b, dt_bias, a_log, d_skip, ssd_norm, batch)
    y_gla = _gla(q, k, v, r, sm, gla_w_gate, gla_b_gate, gla_norm, batch)
    return _out_proj_ln(x2, y_ssd, y_gla, w_out.astype(BF16), ln1_g, ln1_b)


def kernel(x, l0_w_in, l0_conv_w, l0_conv_b, l0_dt_bias, l0_A_log, l0_D, l0_ssd_norm, l0_gla_w_gate, l0_gla_b_gate, l0_gla_norm, l0_w_out, l0_ln1_g, l0_ln1_b, l0_ffn_w_gate, l0_ffn_w_up, l0_ffn_w_down, l0_ln2_g, l0_ln2_b, l1_s5_A_re, l1_s5_A_im, l1_s5_log_step, l1_s5_B_re, l1_s5_B_im, l1_s5_C_re, l1_s5_C_im, l1_s5_D, l1_glu_w_a, l1_glu_b_a, l1_glu_w_b, l1_glu_b_b, l1_ln1_g, l1_ln1_b, l1_ffn_w_gate, l1_ffn_w_up, l1_ffn_w_down, l1_ln2_g, l1_ln2_b):
    batch, seq, d = x.shape
    x2 = x.reshape(batch * seq, d)
    bf = lambda w: w.astype(BF16)
    x2 = _layer0(x2, batch, l0_w_in, l0_conv_w, l0_conv_b, l0_dt_bias, l0_A_log, l0_D, l0_ssd_norm,
                 l0_gla_w_gate, l0_gla_b_gate, l0_gla_norm, l0_w_out, l0_ln1_g, l0_ln1_b)
    x2 = _ffn_ln(x2, bf(l0_ffn_w_gate), bf(l0_ffn_w_up), bf(l0_ffn_w_down), l0_ln2_g, l0_ln2_b)
    y_s5 = _s5(x2, l1_s5_A_re, l1_s5_A_im, l1_s5_log_step, l1_s5_B_re, l1_s5_B_im, l1_s5_C_re, l1_s5_C_im, batch)
    x2 = _glu_ln(x2, y_s5, l1_s5_D, bf(l1_glu_w_a), l1_glu_b_a, bf(l1_glu_w_b), l1_glu_b_b, l1_ln1_g, l1_ln1_b)
    x2 = _ffn_ln(x2, bf(l1_ffn_w_gate), bf(l1_ffn_w_up), bf(l1_ffn_w_down), l1_ln2_g, l1_ln2_b)
    return x2.reshape(batch, seq, d)
```
